```python
import math
import jax, jax.numpy as jnp
from jax import lax
import numpy as np

D_MODEL = 2048
BATCH = 16
SEQ = 2048
DEPTH = 4

A_HEADS = 6
A_HEAD_DIM = 64
A_V_DIM = 2 * A_HEAD_DIM
A_WIDTH = A_HEADS * A_V_DIM
Q_BLOCK = 128
B_GROUPS = ((128, 1), (512, 4), (2048, 16))
B_HEADS_PER_GROUP = 4
B_HEAD_DIM = 64
B_HEADS = B_HEADS_PER_GROUP * len(B_GROUPS)
B_WIDTH = B_HEADS * B_HEAD_DIM
BAND_BLOCK = 64
C_WINDOWS = (2, 4, 8, 16)
C_GROUP_DIM = 128
C_WIDTH = C_GROUP_DIM * len(C_WINDOWS)
N_BRANCHES = 3
D_FF = 4 * D_MODEL
REL_BUCKETS = 32
REL_MAX_DISTANCE = 1024
REL_HEADS = A_HEADS + B_HEADS
RMS_EPS = 1e-6
NEG_INF = -1e30

IN_SIZES = (
    A_HEADS * 2 * A_HEAD_DIM,
    A_HEADS * 2 * A_HEAD_DIM,
    A_WIDTH,
    B_WIDTH,
    B_WIDTH,
    B_WIDTH,
    C_WIDTH,
    N_BRANCHES * D_MODEL,
)
IN_COLS = sum(IN_SIZES)
IN_SPLITS = np.cumsum(IN_SIZES)[:-1].tolist()

kernel_name = "hybrid_gated_diffattn_dilated_pool_encoder"


def rms_norm(x, g):
    xf = x.astype(jnp.float32)
    y = xf * lax.rsqrt(jnp.mean(xf * xf, axis=-1, keepdims=True) + RMS_EPS)
    return (y * g.astype(jnp.float32)).astype(x.dtype)


def t5_bucket(rel):
    half = REL_BUCKETS // 2
    exact = half // 2
    n = jnp.abs(rel)
    sign_off = jnp.where(rel > 0, half, 0)
    nf = jnp.maximum(n, 1).astype(jnp.float32)
    large = exact + (jnp.log(nf / exact) / math.log(REL_MAX_DISTANCE / exact)
                     * (half - exact)).astype(jnp.int32)
    large = jnp.minimum(large, half - 1)
    return sign_off + jnp.where(n < exact, n, large)


def diff_attention(q, k, v, table_a, lam, lam_init, subln_g):
    B, S = q.shape[:2]
    nblk = S // Q_BLOCK
    q_blocks = q.reshape(B, nblk, Q_BLOCK, A_HEADS, 2, A_HEAD_DIM).transpose(1, 0, 2, 3, 4, 5)
    starts = jnp.arange(nblk, dtype=jnp.int32) * Q_BLOCK
    kpos = jnp.arange(S, dtype=jnp.int32)
    scale = A_HEAD_DIM ** -0.5

    def one_block(args):
        qb, start = args
        qpos = start + jnp.arange(Q_BLOCK, dtype=jnp.int32)
        rel = kpos[None, :] - qpos[:, None]
        bias = table_a[t5_bucket(rel)].astype(jnp.float32).transpose(2, 0, 1)
        logits = jnp.einsum('bqhcd,bkhcd->bhcqk', qb, k).astype(jnp.float32) * scale
        p = jax.nn.softmax(logits + bias[None, :, None], axis=-1)
        attn = p[:, :, 0] - lam * p[:, :, 1]
        return jnp.einsum('bhqk,bkhe->bqhe', attn.astype(v.dtype), v)

    out = lax.map(one_block, (q_blocks, starts))
    out = out.transpose(1, 0, 2, 3, 4).reshape(B, S, A_HEADS, A_V_DIM)
    out = rms_norm(out, subln_g) * (1.0 - lam_init)
    return out.reshape(B, S, A_WIDTH)


def dilated_group(q, k, v, table_g, window, dilation):
    B, S, Hg, d = q.shape
    r = dilation
    n = window // (2 * r)
    L = S // r
    bs = math.gcd(L, BAND_BLOCK)
    nb = L // bs
    span = bs + 2 * n
    qs = q.reshape(B, nb, bs, r, Hg, d)
    pad = ((0, 0), (n, n), (0, 0), (0, 0), (0, 0))
    kp = jnp.pad(k.reshape(B, L, r, Hg, d), pad)
    vp = jnp.pad(v.reshape(B, L, r, Hg, d), pad)
    blk_start = jnp.arange(nb, dtype=jnp.int32) * bs
    t_off = jnp.arange(span, dtype=jnp.int32)
    s_off = jnp.arange(bs, dtype=jnp.int32)
    idx = blk_start[:, None] + t_off[None, :]
    kb = kp[:, idx]
    vb = vp[:, idx]
    rel = t_off[None, :] - n - s_off[:, None]
    kj = blk_start[:, None, None] + t_off[None, None, :] - n
    valid = (jnp.abs(rel) <= n)[None] & (kj >= 0) & (kj < L)
    bias = table_g[t5_bucket(rel * r)].astype(jnp.float32).transpose(2, 0, 1)
    logits = jnp.einsum('bnqthd,bnkthd->bnthqk', qs, kb).astype(jnp.float32) * (d ** -0.5)
    logits = jnp.where(valid[None, :, None, None], logits + bias[None, None, None], NEG_INF)
    m = jnp.max(logits, axis=-1, keepdims=True)
    e = jnp.exp(logits - m)
    ssum = jnp.sum(e, axis=-1, keepdims=True)
    out = jnp.einsum('bnthqk,bnkthd->bnqthd', (e / ssum).astype(v.dtype), vb)
    lse = (m + jnp.log(ssum))[..., 0]
    out = out.reshape(B, S, Hg, d)
    lse = lse.transpose(0, 1, 4, 2, 3).reshape(B, S, Hg)
    return out, lse


def dilated_mixture(q, k, v, table_b):
    B, S = q.shape[:2]
    outs, lses = [], []
    for g, (window, dilation) in enumerate(B_GROUPS):
        hs = slice(g * B_HEADS_PER_GROUP, (g + 1) * B_HEADS_PER_GROUP)
        o, s = dilated_group(q[:, :, hs], k[:, :, hs], v[:, :, hs], table_b[:, hs], window, dilation)
        outs.append(o)
        lses.append(s)
    outs = jnp.stack(outs, axis=0)
    alpha = jax.nn.softmax(jnp.stack(lses, axis=0), axis=0)
    mixed = (alpha[..., None] * outs).astype(q.dtype)
    return mixed.transpose(1, 2, 0, 3, 4).reshape(B, S, B_WIDTH)


def pool_mixer(c, pool_w, pool_scale):
    B, S, _ = c.shape
    cf = c.astype(jnp.float32)
    cs = jnp.concatenate([jnp.zeros((B, 1, C_WIDTH), jnp.float32), jnp.cumsum(cf, axis=1)], axis=1)
    pos = jnp.arange(S, dtype=jnp.int32)
    diffs = []
    for g, win in enumerate(C_WINDOWS):
        rad = win // 2
        lo = jnp.clip(pos - rad, 0, S)
        hi = jnp.clip(pos + rad + 1, 0, S)
        sl = slice(g * C_GROUP_DIM, (g + 1) * C_GROUP_DIM)
        seg = cs[:, hi, sl] - cs[:, lo, sl]
        cnt = (hi - lo).astype(jnp.float32)[None, :, None]
        diffs.append(seg / cnt - cf[:, :, sl])
    dpool = jnp.stack(diffs, axis=2).astype(c.dtype)
    y = jnp.einsum('bsgc,gce->bsge', dpool, pool_w).reshape(B, S, C_WIDTH)
    return y * pool_scale


def setup_inputs(seed: int = 0) -> dict:
    key = jax.random.key(seed)
    ks = jax.random.split(key, 20)
    f32 = jnp.float32

    def nrm(k, shape, scale):
        return jax.random.normal(k, shape, f32) * scale

    return {
        "x": nrm(ks[0], (BATCH, SEQ, D_MODEL), 1.0),
        "rel_bias_table": nrm(ks[1], (REL_BUCKETS, REL_HEADS), 0.3),
        "norm1_g": 1.0 + nrm(ks[2], (DEPTH, D_MODEL), 0.02),
        "w_in": nrm(ks[3], (DEPTH, D_MODEL, IN_COLS), D_MODEL ** -0.5),
        "lambda_q1": nrm(ks[4], (DEPTH, A_HEAD_DIM), 0.1),
        "lambda_k1": nrm(ks[5], (DEPTH, A_HEAD_DIM), 0.1),
        "lambda_q2": nrm(ks[6], (DEPTH, A_HEAD_DIM), 0.1),
        "lambda_k2": nrm(ks[7], (DEPTH, A_HEAD_DIM), 0.1),
        "subln_g": 1.0 + nrm(ks[8], (DEPTH, A_V_DIM), 0.02),
        "pool_w": nrm(ks[9], (DEPTH, len(C_WINDOWS), C_GROUP_DIM, C_GROUP_DIM), C_GROUP_DIM ** -0.5),
        "pool_scale": 1.0 + nrm(ks[10], (DEPTH, C_WIDTH), 0.1),
        "w_proj_a": nrm(ks[11], (DEPTH, A_WIDTH, D_MODEL), A_WIDTH ** -0.5),
        "w_proj_b": nrm(ks[12], (DEPTH, B_WIDTH, D_MODEL), B_WIDTH ** -0.5),
        "w_proj_c": nrm(ks[13], (DEPTH, C_WIDTH, D_MODEL), C_WIDTH ** -0.5),
        "w_out": nrm(ks[14], (DEPTH, D_MODEL, D_MODEL), D_MODEL ** -0.5),
        "norm2_g": 1.0 + nrm(ks[15], (DEPTH, D_MODEL), 0.02),
        "w_up": nrm(ks[16], (DEPTH, D_MODEL, D_FF), D_MODEL ** -0.5),
        "w_down": nrm(ks[17], (DEPTH, D_FF, D_MODEL), D_FF ** -0.5),
        "final_g": 1.0 + nrm(ks[18], (D_MODEL,), 0.02),
    }


def reference(x, rel_bias_table, norm1_g, w_in, lambda_q1, lambda_k1, lambda_q2, lambda_k2,
              subln_g, pool_w, pool_scale, w_proj_a, w_proj_b, w_proj_c, w_out,
              norm2_g, w_up, w_down, final_g):
    B, S, D = x.shape
    table_a = rel_bias_table[:, :A_HEADS]
    table_b = rel_bias_table[:, A_HEADS:]
    for l in range(DEPTH):
        h = rms_norm(x, norm1_g[l])
        z = h @ w_in[l]
        qa, ka, va, qb, kb, vb, c, gates = jnp.split(z, IN_SPLITS, axis=-1)
        lam_init = 0.8 - 0.6 * math.exp(-0.3 * l)
        lam = (jnp.exp(jnp.sum(lambda_q1[l] * lambda_k1[l]).astype(jnp.float32))
               - jnp.exp(jnp.sum(lambda_q2[l] * lambda_k2[l]).astype(jnp.float32)) + lam_init)
        ya = diff_attention(qa.reshape(B, S, A_HEADS, 2, A_HEAD_DIM),
                            ka.reshape(B, S, A_HEADS, 2, A_HEAD_DIM),
                            va.reshape(B, S, A_HEADS, A_V_DIM),
                            table_a, lam, lam_init, subln_g[l])
        yb = dilated_mixture(qb.reshape(B, S, B_HEADS, B_HEAD_DIM),
                             kb.reshape(B, S, B_HEADS, B_HEAD_DIM),
                             vb.reshape(B, S, B_HEADS, B_HEAD_DIM), table_b)
        yc = pool_mixer(c, pool_w[l], pool_scale[l])
        g = jax.nn.sigmoid(gates.astype(jnp.float32)).reshape(B, S, N_BRANCHES, D).astype(x.dtype)
        merged = (g[:, :, 0] * (ya @ w_proj_a[l])
                  + g[:, :, 1] * (yb @ w_proj_b[l])
                  + g[:, :, 2] * (yc @ w_proj_c[l]))
        x = x + merged @ w_out[l]
        h2 = rms_norm(x, norm2_g[l])
        u = jax.nn.relu(h2 @ w_up[l])
        x = x + (u * u) @ w_down[l]
    return rms_norm(x, final_g)
```

```python
import functools
import math

import jax
import jax.numpy as jnp
import numpy as np
from jax import lax
from jax.experimental import pallas as pl
from jax.experimental.pallas import tpu as pltpu

F32 = jnp.float32
BF16 = jnp.bfloat16

D_MODEL = 2048
A_HEADS = 6
A_HEAD_DIM = 64
A_V_DIM = 2 * A_HEAD_DIM
A_WIDTH = A_HEADS * A_V_DIM
B_GROUPS = ((128, 1), (512, 4), (2048, 16))
B_HEADS_PER_GROUP = 4
B_HEAD_DIM = 64
B_GROUP_WIDTH = B_HEADS_PER_GROUP * B_HEAD_DIM
B_WIDTH = B_GROUP_WIDTH * len(B_GROUPS)
B_SIDE = 64
C_WINDOWS = (2, 4, 8, 16)
C_GROUP_DIM = 128
C_WIDTH = C_GROUP_DIM * len(C_WINDOWS)
N_BRANCHES = 3
D_FF = 4 * D_MODEL
REL_BUCKETS = 32
REL_MAX_DISTANCE = 1024
RMS_EPS = 1e-6
NEG_INF = -1e30
QA_OFF, KA_OFF, VA_OFF = 0, A_WIDTH, 2 * A_WIDTH
QB_OFF, KB_OFF, VB_OFF = 3 * A_WIDTH, 3 * A_WIDTH + B_WIDTH, 3 * A_WIDTH + 2 * B_WIDTH
C_OFF = 3 * A_WIDTH + 3 * B_WIDTH
GATE_OFF = C_OFF + C_WIDTH
IN_COLS = GATE_OFF + N_BRANCHES * D_MODEL

V7X_LANES = 128
V7X_VMEM_LIMIT_BYTES = 56 * 1024 * 1024
Q_TILE = 128
POOL_PAD = 8
POOL_CHUNK = 256


def _params(sem):
    return pltpu.CompilerParams(dimension_semantics=sem, vmem_limit_bytes=V7X_VMEM_LIMIT_BYTES)


def _t5_bucket_np(rel):
    half = REL_BUCKETS // 2
    exact = half // 2
    n = np.abs(rel)
    sign_off = np.where(rel > 0, half, 0)
    nf = np.maximum(n, 1).astype(np.float32)
    large = exact + (np.log(nf / np.float32(exact)) / np.float32(math.log(REL_MAX_DISTANCE / exact))
                     * np.float32(half - exact)).astype(np.int32)
    large = np.minimum(large, half - 1)
    return (sign_off + np.where(n < exact, n, large)).astype(np.int32)


def _bias_kernel(table_ref, bucket_ref, o_ref, *, head0):
    h = pl.program_id(0) + head0
    bucket = bucket_ref[...]
    acc = jnp.full(bucket.shape, NEG_INF, F32)
    for b in range(REL_BUCKETS):
        acc = jnp.where(bucket == b, table_ref[b, h], acc)
    o_ref[0] = acc


def _bias_tiles(table, bucket, head0, nheads):
    rows, cols = bucket.shape
    return pl.pallas_call(
        functools.partial(_bias_kernel, head0=head0),
        grid=(nheads,),
        in_specs=[pl.BlockSpec(memory_space=pltpu.SMEM),
                  pl.BlockSpec((rows, cols), lambda h: (0, 0))],
        out_specs=pl.BlockSpec((1, rows, cols), lambda h: (h, 0, 0)),
        out_shape=jax.ShapeDtypeStruct((nheads, rows, cols), F32),
        compiler_params=_params(("arbitrary",)),
    )(table, bucket)


def _norm_matmul_kernel(x_ref, g_ref, w_ref, o_ref, h_ref, *, relu2, row_chunk):
    @pl.when(pl.program_id(1) == 0)
    def _():
        def body(c, carry):
            r0 = pl.multiple_of(c * row_chunk, row_chunk)
            x = x_ref[pl.ds(r0, row_chunk), :]
            ms = jnp.mean(x * x, axis=-1, keepdims=True)
            h_ref[pl.ds(r0, row_chunk), :] = (x * lax.rsqrt(ms + RMS_EPS) * g_ref[...]).astype(BF16)
            return carry
        lax.fori_loop(0, x_ref.shape[0] // row_chunk, body, 0)

    acc = jnp.dot(h_ref[...], w_ref[...], preferred_element_type=F32)
    if relu2:
        acc = jnp.maximum(acc, 0.0)
        acc = acc * acc
    o_ref[...] = acc.astype(o_ref.dtype)


def _norm_matmul(x, g, w, *, relu2, tm, tn):
    m, d = x.shape
    n = w.shape[1]
    return pl.pallas_call(
        functools.partial(_norm_matmul_kernel, relu2=relu2, row_chunk=128),
        grid=(m // tm, n // tn),
        in_specs=[pl.BlockSpec((tm, d), lambda i, j: (i, 0)),
                  pl.BlockSpec((1, d), lambda i, j: (0, 0)),
                  pl.BlockSpec((d, tn), lambda i, j: (0, j))],
        out_specs=pl.BlockSpec((tm, tn), lambda i, j: (i, j)),
        out_shape=jax.ShapeDtypeStruct((m, n), BF16),
        scratch_shapes=[pltpu.VMEM((tm, d), BF16)],
        compiler_params=_params(("parallel", "arbitrary")),
    )(x, g, w)


def _residual_matmul_kernel(a_ref, w_ref, x_ref, g_ref, o_ref, *, final_norm):
    k = pl.program_id(1)

    @pl.when(k == 0)
    def _():
        o_ref[...] = x_ref[...]

    o_ref[...] += jnp.dot(a_ref[...], w_ref[...], preferred_element_type=F32)

    if final_norm:
        @pl.when(k == pl.num_programs(1) - 1)
        def _():
            y = o_ref[...]
            ms = jnp.mean(y * y, axis=-1, keepdims=True)
            o_ref[...] = y * lax.rsqrt(ms + RMS_EPS) * g_ref[...]


def _residual_matmul(a, w, x, g, *, final_norm, tm, tk):
    m, kdim = a.shape
    n = w.shape[1]
    return pl.pallas_call(
        functools.partial(_residual_matmul_kernel, final_norm=final_norm),
        grid=(m // tm, kdim // tk),
        in_specs=[pl.BlockSpec((tm, tk), lambda i, k: (i, k)),
                  pl.BlockSpec((tk, n), lambda i, k: (k, 0)),
                  pl.BlockSpec((tm, n), lambda i, k: (i, 0)),
                  pl.BlockSpec((1, n), lambda i, k: (0, 0))],
        out_specs=pl.BlockSpec((tm, n), lambda i, k: (i, 0)),
        out_shape=jax.ShapeDtypeStruct((m, n), F32),
        compiler_params=_params(("parallel", "arbitrary")),
    )(a, w, x, g)


def _diff_attn_kernel(lam_ref, q_ref, k_ref, v_ref, bias_ref, g_ref, o_ref, kt_ref, *, lam_init, seq):
    lq1, lk1, lq2, lk2 = (lam_ref[i:i + 1, :] for i in range(4))
    lam = (jnp.exp(jnp.sum(lq1 * lk1, axis=-1, keepdims=True))
           - jnp.exp(jnp.sum(lq2 * lk2, axis=-1, keepdims=True)) + lam_init)
    kt = k_ref[0].astype(F32).T
    kt_ref[...] = kt.astype(BF16)
    scale = A_HEAD_DIM ** -0.5
    n_tiles = seq // Q_TILE

    def body(i, carry):
        r0 = pl.multiple_of(i * Q_TILE, Q_TILE)
        q = q_ref[0, pl.ds(r0, Q_TILE), :].astype(F32) * scale
        q = q.astype(BF16)
        c0 = pl.multiple_of((n_tiles - 1 - i) * Q_TILE, Q_TILE)
        bias = bias_ref[0, :, pl.ds(c0, seq)]
        v = v_ref[0]
        outs = []
        for c in range(2):
            s = jnp.dot(q[:, c * A_HEAD_DIM:(c + 1) * A_HEAD_DIM],
                        kt_ref[c * A_HEAD_DIM:(c + 1) * A_HEAD_DIM, :],
                        preferred_element_type=F32) + bias
            m = jnp.max(s, axis=-1, keepdims=True)
            e = jnp.exp(s - m)
            l = jnp.sum(e, axis=-1, keepdims=True)
            pv = jnp.dot(e.astype(BF16), v, preferred_element_type=F32)
            outs.append(pv / l)
        o = outs[0] - lam * outs[1]
        ms = jnp.mean(o * o, axis=-1, keepdims=True)
        o = o * lax.rsqrt(ms + RMS_EPS) * g_ref[...] * (1.0 - lam_init)
        o_ref[0, pl.ds(r0, Q_TILE), :] = o.astype(o_ref.dtype)
        return carry

    lax.fori_loop(0, n_tiles, body, 0)


def _diff_attention(z, lam_params, bias_a, subln_g, *, lam_init):
    b, s, _ = z.shape
    blk = lambda off: pl.BlockSpec((1, s, A_V_DIM), lambda h, bi: (bi, 0, off // A_V_DIM + h))
    return pl.pallas_call(
        functools.partial(_diff_attn_kernel, lam_init=lam_init, seq=s),
        grid=(A_HEADS, b),
        in_specs=[pl.BlockSpec((4, A_HEAD_DIM), lambda h, bi: (0, 0)),
                  blk(QA_OFF), blk(KA_OFF), blk(VA_OFF),
                  pl.BlockSpec((1, Q_TILE, 2 * s - Q_TILE), lambda h, bi: (h, 0, 0)),
                  pl.BlockSpec((1, A_V_DIM), lambda h, bi: (0, 0))],
        out_specs=pl.BlockSpec((1, s, A_V_DIM), lambda h, bi: (bi, 0, h)),
        out_shape=jax.ShapeDtypeStruct((b, s, A_WIDTH), BF16),
        scratch_shapes=[pltpu.VMEM((A_V_DIM, s), BF16)],
        compiler_params=_params(("parallel", "arbitrary")),
    )(lam_params, z, z, z, bias_a, subln_g)


def _band_attn_kernel(q_ref, k_ref, v_ref, bias_ref, o_ref, lse_ref, kt_ref, vp_ref, *, sub_len):
    n_tiles = sub_len // Q_TILE
    zeros_k = jnp.zeros((B_GROUP_WIDTH, Q_TILE), BF16)
    kt_ref[:, 0:Q_TILE] = zeros_k
    kt_ref[:, Q_TILE + sub_len:] = zeros_k
    kt_ref[:, Q_TILE:Q_TILE + sub_len] = k_ref[0].astype(F32).T.astype(BF16)
    zeros_v = jnp.zeros((Q_TILE, B_GROUP_WIDTH), BF16)
    vp_ref[0:Q_TILE, :] = zeros_v
    vp_ref[Q_TILE + sub_len:, :] = zeros_v
    vp_ref[Q_TILE:Q_TILE + sub_len, :] = v_ref[0]
    scale = B_HEAD_DIM ** -0.5
    span = 3 * Q_TILE

    def body(i, carry):
        r0 = pl.multiple_of(i * Q_TILE, Q_TILE)
        q = (q_ref[0, pl.ds(r0, Q_TILE), :].astype(F32) * scale).astype(BF16)
        kpos = lax.broadcasted_iota(jnp.int32, (1, span), 1) + (i - 1) * Q_TILE
        in_range = (kpos >= 0) & (kpos < sub_len)
        outs, lses = [], []
        for h in range(B_HEADS_PER_GROUP):
            hs = slice(h * B_HEAD_DIM, (h + 1) * B_HEAD_DIM)
            s = jnp.dot(q[:, hs], kt_ref[hs, pl.ds(r0, span)], preferred_element_type=F32)
            s = jnp.where(in_range, s + bias_ref[h], NEG_INF)
            m = jnp.max(s, axis=-1, keepdims=True)
            e = jnp.exp(s - m)
            l = jnp.sum(e, axis=-1, keepdims=True)
            pv = jnp.dot(e.astype(BF16), vp_ref[pl.ds(r0, span), hs], preferred_element_type=F32)
            outs.append(pv / l)
            lses.append(jnp.broadcast_to(m + jnp.log(l), (Q_TILE, B_HEAD_DIM)))
        o_ref[0, pl.ds(r0, Q_TILE), :] = jnp.concatenate(outs, axis=-1).astype(o_ref.dtype)
        lse_ref[0, pl.ds(r0, Q_TILE), :] = jnp.concatenate(lses, axis=-1)
        return carry

    lax.fori_loop(0, n_tiles, body, 0)


def _band_attention(z, bias_g, group, dilation):
    b, s, zc = z.shape
    r = dilation
    sub_len = s // r
    zr = z.reshape(b, sub_len, r * zc)
    per_t = zc // B_GROUP_WIDTH
    blk = lambda off: pl.BlockSpec((1, sub_len, B_GROUP_WIDTH),
                                   lambda bi, t: (bi, 0, t * per_t + off // B_GROUP_WIDTH + group))
    out_spec = pl.BlockSpec((1, sub_len, B_GROUP_WIDTH), lambda bi, t: (bi, 0, t))
    o, lse = pl.pallas_call(
        functools.partial(_band_attn_kernel, sub_len=sub_len),
        grid=(b, r),
        in_specs=[blk(QB_OFF), blk(KB_OFF), blk(VB_OFF),
                  pl.BlockSpec((B_HEADS_PER_GROUP, Q_TILE, 3 * Q_TILE), lambda bi, t: (0, 0, 0))],
        out_specs=[out_spec, out_spec],
        out_shape=[jax.ShapeDtypeStruct((b, sub_len, r * B_GROUP_WIDTH), BF16),
                   jax.ShapeDtypeStruct((b, sub_len, r * B_GROUP_WIDTH), F32)],
        scratch_shapes=[pltpu.VMEM((B_GROUP_WIDTH, sub_len + 2 * Q_TILE), BF16),
                        pltpu.VMEM((sub_len + 2 * Q_TILE, B_GROUP_WIDTH), BF16)],
        compiler_params=_params(("parallel", "arbitrary")),
    )(zr, zr, zr, bias_g)
    return o.reshape(b, s, B_GROUP_WIDTH), lse.reshape(b, s, B_GROUP_WIDTH)


def _pool_kernel(c_ref, w_ref, scale_ref, o_ref, cp_ref, *, seq):
    zeros = jnp.zeros((POOL_PAD, C_WIDTH), F32)
    cp_ref[0:POOL_PAD, :] = zeros
    cp_ref[POOL_PAD + seq:, :] = zeros
    cp_ref[POOL_PAD:POOL_PAD + seq, :] = c_ref[0].astype(F32)
    for chunk in range(seq // POOL_CHUNK):
        r0 = chunk * POOL_CHUNK
        pos = lax.broadcasted_iota(jnp.int32, (POOL_CHUNK, 1), 0) + r0
        ys = []
        for g, win in enumerate(C_WINDOWS):
            rad = win // 2
            cols = slice(g * C_GROUP_DIM, (g + 1) * C_GROUP_DIM)
            seg = cp_ref[POOL_PAD + r0 - rad:POOL_PAD + r0 - rad + POOL_CHUNK, cols]
            for j in range(-rad + 1, rad + 1):
                seg = seg + cp_ref[POOL_PAD + r0 + j:POOL_PAD + r0 + j + POOL_CHUNK, cols]
            cnt = (jnp.minimum(pos + rad + 1, seq) - jnp.maximum(pos - rad, 0)).astype(F32)
            centre = cp_ref[POOL_PAD + r0:POOL_PAD + r0 + POOL_CHUNK, cols]
            diff = seg / cnt - centre
            ys.append(jnp.dot(diff.astype(BF16), w_ref[g], preferred_element_type=F32))
        y = jnp.concatenate(ys, axis=-1) * scale_ref[...]
        o_ref[0, r0:r0 + POOL_CHUNK, :] = y.astype(o_ref.dtype)


def _pool_mixer(z, pool_w, pool_scale):
    b, s, _ = z.shape
    return pl.pallas_call(
        functools.partial(_pool_kernel, seq=s),
        grid=(b,),
        in_specs=[pl.BlockSpec((1, s, C_WIDTH), lambda bi: (bi, 0, C_OFF // C_WIDTH)),
                  pl.BlockSpec((len(C_WINDOWS), C_GROUP_DIM, C_GROUP_DIM), lambda bi: (0, 0, 0)),
                  pl.BlockSpec((1, C_WIDTH), lambda bi: (0, 0))],
        out_specs=pl.BlockSpec((1, s, C_WIDTH), lambda bi: (bi, 0, 0)),
        out_shape=jax.ShapeDtypeStruct((b, s, C_WIDTH), BF16),
        scratch_shapes=[pltpu.VMEM((s + 2 * POOL_PAD, C_WIDTH), F32)],
        compiler_params=_params(("parallel",)),
    )(z, pool_w, pool_scale)


def _merge_kernel(ya_ref, ob0_ref, ob1_ref, ob2_ref, l0_ref, l1_ref, l2_ref, yc_ref,
                  wa_ref, wb_ref, wc_ref, g0_ref, g1_ref, g2_ref, o_ref, yb_ref):
    @pl.when(pl.program_id(1) == 0)
    def _():
        l0, l1, l2 = l0_ref[...], l1_ref[...], l2_ref[...]
        mx = jnp.maximum(jnp.maximum(l0, l1), l2)
        e0, e1, e2 = jnp.exp(l0 - mx), jnp.exp(l1 - mx), jnp.exp(l2 - mx)
        tot = e0 + e1 + e2
        for gi, (e, ob) in enumerate(((e0, ob0_ref), (e1, ob1_ref), (e2, ob2_ref))):
            yb_ref[:, gi * B_GROUP_WIDTH:(gi + 1) * B_GROUP_WIDTH] = (
                (e / tot) * ob[...].astype(F32)).astype(BF16)

    def gate(ref):
        return jax.nn.sigmoid(ref[...].astype(F32))

    acc = gate(g0_ref) * jnp.dot(ya_ref[...], wa_ref[...], preferred_element_type=F32)
    acc += gate(g1_ref) * jnp.dot(yb_ref[...], wb_ref[...], preferred_element_type=F32)
    acc += gate(g2_ref) * jnp.dot(yc_ref[...], wc_ref[...], preferred_element_type=F32)
    o_ref[...] = acc.astype(o_ref.dtype)


def _gated_merge(z2, ya, obs, lses, yc, wa, wb, wc, *, tm, tn):
    m = ya.shape[0]
    n = wa.shape[1]
    row = lambda width: pl.BlockSpec((tm, width), lambda i, j: (i, 0))
    wcol = lambda kdim: pl.BlockSpec((kdim, tn), lambda i, j: (0, j))
    gate = lambda br: pl.BlockSpec((tm, tn), lambda i, j: (i, (GATE_OFF + br * n) // tn + j))
    return pl.pallas_call(
        _merge_kernel,
        grid=(m // tm, n // tn),
        in_specs=[row(A_WIDTH)] + [row(B_GROUP_WIDTH)] * 6 + [row(C_WIDTH),
                  wcol(A_WIDTH), wcol(B_WIDTH), wcol(C_WIDTH), gate(0), gate(1), gate(2)],
        out_specs=pl.BlockSpec((tm, tn), lambda i, j: (i, j)),
        out_shape=jax.ShapeDtypeStruct((m, n), BF16),
        scratch_shapes=[pltpu.VMEM((tm, B_WIDTH), BF16)],
        compiler_params=_params(("parallel", "arbitrary")),
    )(ya, *obs, *lses, yc, wa, wb, wc, z2, z2, z2)


def _bias_buckets(seq):
    r = np.arange(Q_TILE, dtype=np.int64)[:, None]
    c = np.arange(2 * seq - Q_TILE, dtype=np.int64)[None, :]
    bucket_a = _t5_bucket_np(c - r - (seq - Q_TILE))
    c3 = np.arange(3 * Q_TILE, dtype=np.int64)[None, :]
    rel = c3 - Q_TILE - r
    buckets_b = []
    for _, dilation in B_GROUPS:
        bk = _t5_bucket_np(rel * dilation)
        buckets_b.append(np.where(np.abs(rel) <= B_SIDE, bk, -1).astype(np.int32))
    return bucket_a, buckets_b


def kernel(x, rel_bias_table, norm1_g, w_in, lambda_q1, lambda_k1, lambda_q2, lambda_k2, subln_g,
           pool_w, pool_scale, w_proj_a, w_proj_b, w_proj_c, w_out, norm2_g, w_up, w_down, final_g):
    b, s, d = x.shape
    depth = w_in.shape[0]
    m = b * s
    assert d == D_MODEL and s % (Q_TILE * B_GROUPS[-1][1]) == 0 and s % POOL_CHUNK == 0

    bucket_a, buckets_b = _bias_buckets(s)
    table = rel_bias_table.astype(F32)
    bias_a = _bias_tiles(table, jnp.asarray(bucket_a), 0, A_HEADS)
    bias_b = [_bias_tiles(table, jnp.asarray(bk), A_HEADS + gi * B_HEADS_PER_GROUP, B_HEADS_PER_GROUP)
              for gi, bk in enumerate(buckets_b)]

    xs = x.reshape(m, d)
    for l in range(depth):
        lam_init = 0.8 - 0.6 * math.exp(-0.3 * l)
        z2 = _norm_matmul(xs, norm1_g[l].reshape(1, d), w_in[l].astype(BF16), relu2=False, tm=1024, tn=1024)
        z = z2.reshape(b, s, IN_COLS)
        lam_params = jnp.stack([lambda_q1[l], lambda_k1[l], lambda_q2[l], lambda_k2[l]]).astype(F32)
        ya = _diff_attention(z, lam_params, bias_a, subln_g[l].reshape(1, A_V_DIM), lam_init=lam_init)
        obs, lses = [], []
        for gi, (_, dilation) in enumerate(B_GROUPS):
            o_g, lse_g = _band_attention(z, bias_b[gi], gi, dilation)
            obs.append(o_g.reshape(m, B_GROUP_WIDTH))
            lses.append(lse_g.reshape(m, B_GROUP_WIDTH))
        yc = _pool_mixer(z, pool_w[l].astype(BF16), pool_scale[l].reshape(1, C_WIDTH))
        merged = _gated_merge(z2, ya.reshape(m, A_WIDTH), obs, lses, yc.reshape(m, C_WIDTH),
                              w_proj_a[l].astype(BF16), w_proj_b[l].astype(BF16), w_proj_c[l].astype(BF16),
                              tm=512, tn=1024)
        xs = _residual_matmul(merged, w_out[l].astype(BF16), xs, final_g.reshape(1, d),
                              final_norm=False, tm=512, tk=D_MODEL)
        u = _norm_matmul(xs, norm2_g[l].reshape(1, d), w_up[l].astype(BF16), relu2=True, tm=1024, tn=1024)
        xs = _residual_matmul(u, w_down[l].astype(BF16), xs, final_g.reshape(1, d),
                              final_norm=(l == depth - 1), tm=512, tk=D_MODEL)
    return xs.reshape(b, s, d)
```

```python
import functools
import math

import jax
import jax.numpy as jnp
import numpy as np
from jax import lax
from jax.experimental import pallas as pl
from jax.experimental.pallas import tpu as pltpu

F32 = jnp.float32
BF16 = jnp.bfloat16

D_MODEL = 2048
A_HEADS = 6
A_HEAD_DIM = 64
A_V_DIM = 2 * A_HEAD_DIM
A_WIDTH = A_HEADS * A_V_DIM
B_GROUPS = ((128, 1), (512, 4), (2048, 16))
B_HEADS_PER_GROUP = 4
B_HEAD_DIM = 64
B_GROUP_WIDTH = B_HEADS_PER_GROUP * B_HEAD_DIM
B_WIDTH = B_GROUP_WIDTH * len(B_GROUPS)
B_SIDE = 64
C_WINDOWS = (2, 4, 8, 16)
C_GROUP_DIM = 128
C_WIDTH = C_GROUP_DIM * len(C_WINDOWS)
N_BRANCHES = 3
D_FF = 4 * D_MODEL
REL_BUCKETS = 32
REL_MAX_DISTANCE = 1024
RMS_EPS = 1e-6
NEG_INF = -1e30
LOG2E = math.log2(math.e)
LN2 = math.log(2.0)
QA_OFF, KA_OFF, VA_OFF = 0, A_WIDTH, 2 * A_WIDTH
QB_OFF, KB_OFF, VB_OFF = 3 * A_WIDTH, 3 * A_WIDTH + B_WIDTH, 3 * A_WIDTH + 2 * B_WIDTH
C_OFF = 3 * A_WIDTH + 3 * B_WIDTH
GATE_OFF = C_OFF + C_WIDTH
IN_COLS = GATE_OFF + N_BRANCHES * D_MODEL

V7X_VMEM_LIMIT_BYTES = 56 * 1024 * 1024
Q_TILE = 128
POOL_PAD = 8
POOL_CHUNK = 256


def _params(sem):
    return pltpu.CompilerParams(dimension_semantics=sem, vmem_limit_bytes=V7X_VMEM_LIMIT_BYTES)


def _t5_bucket_np(rel):
    half = REL_BUCKETS // 2
    exact = half // 2
    n = np.abs(rel)
    sign_off = np.where(rel > 0, half, 0)
    nf = np.maximum(n, 1).astype(np.float32)
    large = exact + (np.log(nf / np.float32(exact)) / np.float32(math.log(REL_MAX_DISTANCE / exact))
                     * np.float32(half - exact)).astype(np.int32)
    large = np.minimum(large, half - 1)
    return (sign_off + np.where(n < exact, n, large)).astype(np.int32)


def _bias_kernel(table_ref, bucket_ref, o_ref, *, head0):
    h = pl.program_id(0) + head0
    bucket = bucket_ref[...]
    acc = jnp.full(bucket.shape, NEG_INF, F32)
    for b in range(REL_BUCKETS):
        acc = jnp.where(bucket == b, table_ref[b, h] * LOG2E, acc)
    o_ref[0] = acc


def _bias_tiles(table, bucket, head0, nheads):
    rows, cols = bucket.shape
    return pl.pallas_call(
        functools.partial(_bias_kernel, head0=head0),
        grid=(nheads,),
        in_specs=[pl.BlockSpec(memory_space=pltpu.SMEM),
                  pl.BlockSpec((rows, cols), lambda h: (0, 0))],
        out_specs=pl.BlockSpec((1, rows, cols), lambda h: (h, 0, 0)),
        out_shape=jax.ShapeDtypeStruct((nheads, rows, cols), F32),
        compiler_params=_params(("arbitrary",)),
        name="rel_bias_tiles",
    )(table, bucket)


def _norm_matmul_kernel(x_ref, g_ref, w_ref, cs_ref, o_ref, h_ref, *, relu2, col_scale, row_chunk):
    @pl.when(pl.program_id(1) == 0)
    def _():
        def body(c, carry):
            r0 = pl.multiple_of(c * row_chunk, row_chunk)
            x = x_ref[pl.ds(r0, row_chunk), :]
            ms = jnp.mean(x * x, axis=-1, keepdims=True)
            h_ref[pl.ds(r0, row_chunk), :] = (x * lax.rsqrt(ms + RMS_EPS) * g_ref[...]).astype(BF16)
            return carry
        lax.fori_loop(0, x_ref.shape[0] // row_chunk, body, 0)

    acc = jnp.dot(h_ref[...], w_ref[...], preferred_element_type=F32)
    if relu2:
        acc = jnp.maximum(acc, 0.0)
        acc = acc * acc
    if col_scale:
        acc = acc * cs_ref[...]
    o_ref[...] = acc.astype(o_ref.dtype)


def _norm_matmul(x, g, w, cs, *, relu2, col_scale, tm, tn, name):
    m, d = x.shape
    n = w.shape[1]
    return pl.pallas_call(
        functools.partial(_norm_matmul_kernel, relu2=relu2, col_scale=col_scale, row_chunk=128),
        grid=(m // tm, n // tn),
        in_specs=[pl.BlockSpec((tm, d), lambda i, j: (i, 0)),
                  pl.BlockSpec((1, d), lambda i, j: (0, 0)),
                  pl.BlockSpec((d, tn), lambda i, j: (0, j)),
                  pl.BlockSpec((1, tn), lambda i, j: (0, j))],
        out_specs=pl.BlockSpec((tm, tn), lambda i, j: (i, j)),
        out_shape=jax.ShapeDtypeStruct((m, n), BF16),
        scratch_shapes=[pltpu.VMEM((tm, d), BF16)],
        compiler_params=_params(("parallel", "arbitrary")),
        name=name,
    )(x, g, w, cs)


def _residual_matmul_kernel(a_ref, w_ref, x_ref, g_ref, o_ref, *, final_norm):
    k = pl.program_id(1)

    @pl.when(k == 0)
    def _():
        o_ref[...] = x_ref[...]

    o_ref[...] += jnp.dot(a_ref[...], w_ref[...], preferred_element_type=F32)

    if final_norm:
        @pl.when(k == pl.num_programs(1) - 1)
        def _():
            y = o_ref[...]
            ms = jnp.mean(y * y, axis=-1, keepdims=True)
            o_ref[...] = y * lax.rsqrt(ms + RMS_EPS) * g_ref[...]


def _residual_matmul(a, w, x, g, *, final_norm, tm, tk, name):
    m, kdim = a.shape
    n = w.shape[1]
    return pl.pallas_call(
        functools.partial(_residual_matmul_kernel, final_norm=final_norm),
        grid=(m // tm, kdim // tk),
        in_specs=[pl.BlockSpec((tm, tk), lambda i, k: (i, k)),
                  pl.BlockSpec((tk, n), lambda i, k: (k, 0)),
                  pl.BlockSpec((tm, n), lambda i, k: (i, 0)),
                  pl.BlockSpec((1, n), lambda i, k: (0, 0))],
        out_specs=pl.BlockSpec((tm, n), lambda i, k: (i, 0)),
        out_shape=jax.ShapeDtypeStruct((m, n), F32),
        compiler_params=_params(("parallel", "arbitrary")),
        name=name,
    )(a, w, x, g)


def _diff_attn_kernel(lam_ref, q_ref, k_ref, v_ref, bias_ref, g_ref, o_ref, qt_ref, vt_ref, s_ref, m_ref,
                      *, lam_init, seq):
    lq1, lk1, lq2, lk2 = (lam_ref[i:i + 1, :] for i in range(4))
    lam = (jnp.exp(jnp.sum(lq1 * lk1, axis=-1, keepdims=True))
           - jnp.exp(jnp.sum(lq2 * lk2, axis=-1, keepdims=True)) + lam_init)
    qt_ref[...] = q_ref[0].astype(F32).T.astype(BF16)
    vt_ref[...] = v_ref[0].astype(F32).T.astype(BF16)
    n_tiles = seq // Q_TILE
    comp0_rows = lax.broadcasted_iota(jnp.int32, (A_V_DIM, Q_TILE), 0) < A_HEAD_DIM

    def logits_stage(i, slot):
        r0 = pl.multiple_of(i * Q_TILE, Q_TILE)
        qt = qt_ref[:, pl.ds(r0, Q_TILE)]
        zero = jnp.zeros_like(qt)
        w = jnp.concatenate([jnp.where(comp0_rows, qt, zero), jnp.where(comp0_rows, zero, qt)], axis=1)
        c0 = pl.multiple_of((n_tiles - 1 - i) * Q_TILE, Q_TILE)
        bias = bias_ref[0, pl.ds(c0, seq), :]
        s = jnp.dot(k_ref[0], w, preferred_element_type=F32)
        s = s + jnp.concatenate([bias, bias], axis=1)
        s_ref[slot] = s
        m_ref[slot] = jnp.max(s, axis=0, keepdims=True)

    def value_stage(i, slot):
        r0 = pl.multiple_of(i * Q_TILE, Q_TILE)
        e = jnp.exp2(s_ref[slot] - m_ref[slot])
        l = jnp.sum(e, axis=0, keepdims=True)
        pv = jnp.dot(vt_ref[...], e.astype(BF16), preferred_element_type=F32) / l
        o = pv[:, :Q_TILE] - lam * pv[:, Q_TILE:]
        ms = jnp.mean(o * o, axis=0, keepdims=True)
        o = o * lax.rsqrt(ms + RMS_EPS) * g_ref[...] * (1.0 - lam_init)
        o_ref[0, pl.ds(r0, Q_TILE), :] = o.T.astype(o_ref.dtype)

    logits_stage(0, 0)

    def body(j, carry):
        i0 = 2 * j
        logits_stage(i0 + 1, 1)
        value_stage(i0, 0)
        logits_stage(jnp.minimum(i0 + 2, n_tiles - 1), 0)
        value_stage(i0 + 1, 1)
        return carry

    lax.fori_loop(0, n_tiles // 2, body, 0)


def _diff_attention(z, lam_params, bias_a, subln_g, *, lam_init):
    b, s, _ = z.shape
    blk = lambda off: pl.BlockSpec((1, s, A_V_DIM), lambda h, bi: (bi, 0, off // A_V_DIM + h))
    return pl.pallas_call(
        functools.partial(_diff_attn_kernel, lam_init=lam_init, seq=s),
        grid=(A_HEADS, b),
        in_specs=[pl.BlockSpec((4, A_HEAD_DIM), lambda h, bi: (0, 0)),
                  blk(QA_OFF), blk(KA_OFF), blk(VA_OFF),
                  pl.BlockSpec((1, 2 * s - Q_TILE, Q_TILE), lambda h, bi: (h, 0, 0)),
                  pl.BlockSpec((A_V_DIM, 1), lambda h, bi: (0, 0))],
        out_specs=pl.BlockSpec((1, s, A_V_DIM), lambda h, bi: (bi, 0, h)),
        out_shape=jax.ShapeDtypeStruct((b, s, A_WIDTH), BF16),
        scratch_shapes=[pltpu.VMEM((A_V_DIM, s), BF16),
                        pltpu.VMEM((A_V_DIM, s), BF16),
                        pltpu.VMEM((2, s, 2 * Q_TILE), F32),
                        pltpu.VMEM((2, 1, 2 * Q_TILE), F32)],
        compiler_params=_params(("parallel", "arbitrary")),
        name="diff_attention",
    )(lam_params, z, z, z, bias_a, subln_g)


def _band_attn_kernel(q_ref, k_ref, v_ref, bias_ref, o_ref, lse_ref, kt_ref, vp_ref, *, sub_len):
    n_tiles = sub_len // Q_TILE
    zeros_k = jnp.zeros((B_GROUP_WIDTH, Q_TILE), BF16)
    kt_ref[:, 0:Q_TILE] = zeros_k
    kt_ref[:, Q_TILE + sub_len:] = zeros_k
    kt_ref[:, Q_TILE:Q_TILE + sub_len] = k_ref[0].astype(F32).T.astype(BF16)
    zeros_v = jnp.zeros((Q_TILE, B_GROUP_WIDTH), BF16)
    vp_ref[0:Q_TILE, :] = zeros_v
    vp_ref[Q_TILE + sub_len:, :] = zeros_v
    vp_ref[Q_TILE:Q_TILE + sub_len, :] = v_ref[0]
    span = 3 * Q_TILE

    def body(i, carry):
        r0 = pl.multiple_of(i * Q_TILE, Q_TILE)
        q = q_ref[0, pl.ds(r0, Q_TILE), :]
        kpos = lax.broadcasted_iota(jnp.int32, (1, span), 1) + (i - 1) * Q_TILE
        in_range = (kpos >= 0) & (kpos < sub_len)
        outs, lses = [], []
        for h in range(B_HEADS_PER_GROUP):
            hs = slice(h * B_HEAD_DIM, (h + 1) * B_HEAD_DIM)
            s = jnp.dot(q[:, hs], kt_ref[hs, pl.ds(r0, span)], preferred_element_type=F32)
            s = jnp.where(in_range, s + bias_ref[h], NEG_INF)
            m = jnp.max(s, axis=-1, keepdims=True)
            e = jnp.exp2(s - m)
            l = jnp.sum(e, axis=-1, keepdims=True)
            pv = jnp.dot(e.astype(BF16), vp_ref[pl.ds(r0, span), hs], preferred_element_type=F32)
            outs.append(pv / l)
            lses.append(jnp.broadcast_to(m * LN2 + jnp.log(l), (Q_TILE, B_HEAD_DIM)))
        o_ref[0, pl.ds(r0, Q_TILE), :] = jnp.concatenate(outs, axis=-1).astype(o_ref.dtype)
        lse_ref[0, pl.ds(r0, Q_TILE), :] = jnp.concatenate(lses, axis=-1)
        return carry

    lax.fori_loop(0, n_tiles, body, 0)


def _band_attention(src, q_blk, k_blk, v_blk, bias_g, dilation, name):
    b, sub_len, _ = src.shape
    r = dilation
    blk = lambda f: pl.BlockSpec((1, sub_len, B_GROUP_WIDTH), lambda bi, t: (bi, 0, f(t)))
    out_spec = pl.BlockSpec((1, sub_len, B_GROUP_WIDTH), lambda bi, t: (bi, 0, t))
    o, lse = pl.pallas_call(
        functools.partial(_band_attn_kernel, sub_len=sub_len),
        grid=(b, r),
        in_specs=[blk(q_blk), blk(k_blk), blk(v_blk),
                  pl.BlockSpec((B_HEADS_PER_GROUP, Q_TILE, 3 * Q_TILE), lambda bi, t: (0, 0, 0))],
        out_specs=[out_spec, out_spec],
        out_shape=[jax.ShapeDtypeStruct((b, sub_len, r * B_GROUP_WIDTH), BF16),
                   jax.ShapeDtypeStruct((b, sub_len, r * B_GROUP_WIDTH), F32)],
        scratch_shapes=[pltpu.VMEM((B_GROUP_WIDTH, sub_len + 2 * Q_TILE), BF16),
                        pltpu.VMEM((sub_len + 2 * Q_TILE, B_GROUP_WIDTH), BF16)],
        compiler_params=_params(("parallel", "arbitrary")),
        name=name,
    )(src, src, src, bias_g)
    return o.reshape(b, sub_len * r, B_GROUP_WIDTH), lse.reshape(b, sub_len * r, B_GROUP_WIDTH)


def _band_group(z, bias_g, group, dilation):
    b, s, zc = z.shape
    w = B_GROUP_WIDTH
    if dilation == 1:
        first = lambda off: off // w + group
        return _band_attention(z, lambda t: first(QB_OFF), lambda t: first(KB_OFF), lambda t: first(VB_OFF),
                               bias_g, 1, "band_attention_d1")
    cols = [lax.slice_in_dim(z, off + group * w, off + (group + 1) * w, axis=2) for off in (QB_OFF, KB_OFF, VB_OFF)]
    src = jnp.concatenate(cols, axis=2).reshape(b, s // dilation, dilation * 3 * w)
    return _band_attention(src, lambda t: 3 * t, lambda t: 3 * t + 1, lambda t: 3 * t + 2,
                           bias_g, dilation, f"band_attention_d{dilation}")


def _pool_kernel(c_ref, w_ref, scale_ref, o_ref, cp_ref, *, seq):
    zeros = jnp.zeros((POOL_PAD, C_WIDTH), F32)
    cp_ref[0:POOL_PAD, :] = zeros
    cp_ref[POOL_PAD + seq:, :] = zeros
    cp_ref[POOL_PAD:POOL_PAD + seq, :] = c_ref[0].astype(F32)
    for chunk in range(seq // POOL_CHUNK):
        r0 = chunk * POOL_CHUNK
        pos = lax.broadcasted_iota(jnp.int32, (POOL_CHUNK, 1), 0) + r0
        ys = []
        for g, win in enumerate(C_WINDOWS):
            rad = win // 2
            cols = slice(g * C_GROUP_DIM, (g + 1) * C_GROUP_DIM)
            seg = cp_ref[POOL_PAD + r0 - rad:POOL_PAD + r0 - rad + POOL_CHUNK, cols]
            for j in range(-rad + 1, rad + 1):
                seg = seg + cp_ref[POOL_PAD + r0 + j:POOL_PAD + r0 + j + POOL_CHUNK, cols]
            cnt = (jnp.minimum(pos + rad + 1, seq) - jnp.maximum(pos - rad, 0)).astype(F32)
            centre = cp_ref[POOL_PAD + r0:POOL_PAD + r0 + POOL_CHUNK, cols]
            diff = seg / cnt - centre
            ys.append(jnp.dot(diff.astype(BF16), w_ref[g], preferred_element_type=F32))
        y = jnp.concatenate(ys, axis=-1) * scale_ref[...]
        o_ref[0, r0:r0 + POOL_CHUNK, :] = y.astype(o_ref.dtype)


def _pool_mixer(z, pool_w, pool_scale):
    b, s, _ = z.shape
    return pl.pallas_call(
        functools.partial(_pool_kernel, seq=s),
        grid=(b,),
        in_specs=[pl.BlockSpec((1, s, C_WIDTH), lambda bi: (bi, 0, C_OFF // C_WIDTH)),
                  pl.BlockSpec((len(C_WINDOWS), C_GROUP_DIM, C_GROUP_DIM), lambda bi: (0, 0, 0)),
                  pl.BlockSpec((1, C_WIDTH), lambda bi: (0, 0))],
        out_specs=pl.BlockSpec((1, s, C_WIDTH), lambda bi: (bi, 0, 0)),
        out_shape=jax.ShapeDtypeStruct((b, s, C_WIDTH), BF16),
        scratch_shapes=[pltpu.VMEM((s + 2 * POOL_PAD, C_WIDTH), F32)],
        compiler_params=_params(("parallel",)),
        name="pool_mixer",
    )(z, pool_w, pool_scale)


def _merge_kernel(ya_ref, ob0_ref, ob1_ref, ob2_ref, l0_ref, l1_ref, l2_ref, yc_ref,
                  wa_ref, wb_ref, wc_ref, g0_ref, g1_ref, g2_ref, o_ref, yb_ref):
    @pl.when(pl.program_id(1) == 0)
    def _():
        l0, l1, l2 = l0_ref[...], l1_ref[...], l2_ref[...]
        mx = jnp.maximum(jnp.maximum(l0, l1), l2)
        e0, e1, e2 = jnp.exp(l0 - mx), jnp.exp(l1 - mx), jnp.exp(l2 - mx)
        tot = e0 + e1 + e2
        for gi, (e, ob) in enumerate(((e0, ob0_ref), (e1, ob1_ref), (e2, ob2_ref))):
            yb_ref[:, gi * B_GROUP_WIDTH:(gi + 1) * B_GROUP_WIDTH] = (
                (e / tot) * ob[...].astype(F32)).astype(BF16)

    def gate(ref):
        return 0.5 * jnp.tanh(0.5 * ref[...].astype(F32)) + 0.5

    acc = gate(g0_ref) * jnp.dot(ya_ref[...], wa_ref[...], preferred_element_type=F32)
    acc += gate(g1_ref) * jnp.dot(yb_ref[...], wb_ref[...], preferred_element_type=F32)
    acc += gate(g2_ref) * jnp.dot(yc_ref[...], wc_ref[...], preferred_element_type=F32)
    o_ref[...] = acc.astype(o_ref.dtype)


def _gated_merge(z2, ya, obs, lses, yc, wa, wb, wc, *, tm, tn):
    m = ya.shape[0]
    n = wa.shape[1]
    row = lambda width: pl.BlockSpec((tm, width), lambda i, j: (i, 0))
    wcol = lambda kdim: pl.BlockSpec((kdim, tn), lambda i, j: (0, j))
    gate = lambda br: pl.BlockSpec((tm, tn), lambda i, j: (i, (GATE_OFF + br * n) // tn + j))
    return pl.pallas_call(
        _merge_kernel,
        grid=(m // tm, n // tn),
        in_specs=[row(A_WIDTH)] + [row(B_GROUP_WIDTH)] * 6 + [row(C_WIDTH),
                  wcol(A_WIDTH), wcol(B_WIDTH), wcol(C_WIDTH), gate(0), gate(1), gate(2)],
        out_specs=pl.BlockSpec((tm, tn), lambda i, j: (i, j)),
        out_shape=jax.ShapeDtypeStruct((m, n), BF16),
        scratch_shapes=[pltpu.VMEM((tm, B_WIDTH), BF16)],
        compiler_params=_params(("parallel", "arbitrary")),
        name="gated_merge",
    )(ya, *obs, *lses, yc, wa, wb, wc, z2, z2, z2)


def _bias_buckets(seq):
    r = np.arange(Q_TILE, dtype=np.int64)[:, None]
    c = np.arange(2 * seq - Q_TILE, dtype=np.int64)[None, :]
    bucket_a = np.ascontiguousarray(_t5_bucket_np(c - r - (seq - Q_TILE)).T)
    c3 = np.arange(3 * Q_TILE, dtype=np.int64)[None, :]
    rel = c3 - Q_TILE - r
    buckets_b = []
    for _, dilation in B_GROUPS:
        bk = _t5_bucket_np(rel * dilation)
        buckets_b.append(np.where(np.abs(rel) <= B_SIDE, bk, -1).astype(np.int32))
    return bucket_a, buckets_b


def _in_proj_col_scale():
    cs = np.ones((1, IN_COLS), np.float32)
    cs[0, QA_OFF:QA_OFF + A_WIDTH] = A_HEAD_DIM ** -0.5 * LOG2E
    cs[0, QB_OFF:QB_OFF + B_WIDTH] = B_HEAD_DIM ** -0.5 * LOG2E
    return jnp.asarray(cs)


def kernel(x, rel_bias_table, norm1_g, w_in, lambda_q1, lambda_k1, lambda_q2, lambda_k2, subln_g,
           pool_w, pool_scale, w_proj_a, w_proj_b, w_proj_c, w_out, norm2_g, w_up, w_down, final_g):
    b, s, d = x.shape
    depth = w_in.shape[0]
    m = b * s
    assert d == D_MODEL and s % (Q_TILE * B_GROUPS[-1][1]) == 0 and s % POOL_CHUNK == 0

    bucket_a, buckets_b = _bias_buckets(s)
    table = rel_bias_table.astype(F32)
    bias_a = _bias_tiles(table, jnp.asarray(bucket_a), 0, A_HEADS)
    bias_b = [_bias_tiles(table, jnp.asarray(bk), A_HEADS + gi * B_HEADS_PER_GROUP, B_HEADS_PER_GROUP)
              for gi, bk in enumerate(buckets_b)]
    in_scale = _in_proj_col_scale()
    ff_scale = jnp.ones((1, D_FF), F32)

    xs = x.reshape(m, d)
    for l in range(depth):
        lam_init = 0.8 - 0.6 * math.exp(-0.3 * l)
        z2 = _norm_matmul(xs, norm1_g[l].reshape(1, d), w_in[l].astype(BF16), in_scale,
                          relu2=False, col_scale=True, tm=1024, tn=1024, name="norm_in_proj")
        z = z2.reshape(b, s, IN_COLS)
        lam_params = jnp.stack([lambda_q1[l], lambda_k1[l], lambda_q2[l], lambda_k2[l]]).astype(F32)
        ya = _diff_attention(z, lam_params, bias_a, subln_g[l].reshape(A_V_DIM, 1), lam_init=lam_init)
        obs, lses = [], []
        for gi, (_, dilation) in enumerate(B_GROUPS):
            o_g, lse_g = _band_group(z, bias_b[gi], gi, dilation)
            obs.append(o_g.reshape(m, B_GROUP_WIDTH))
            lses.append(lse_g.reshape(m, B_GROUP_WIDTH))
        yc = _pool_mixer(z, pool_w[l].astype(BF16), pool_scale[l].reshape(1, C_WIDTH))
        merged = _gated_merge(z2, ya.reshape(m, A_WIDTH), obs, lses, yc.reshape(m, C_WIDTH),
                              w_proj_a[l].astype(BF16), w_proj_b[l].astype(BF16), w_proj_c[l].astype(BF16),
                              tm=512, tn=1024)
        xs = _residual_matmul(merged, w_out[l].astype(BF16), xs, final_g.reshape(1, d),
                              final_norm=False, tm=512, tk=D_MODEL, name="out_proj_residual")
        u = _norm_matmul(xs, norm2_g[l].reshape(1, d), w_up[l].astype(BF16), ff_scale,
                         relu2=True, col_scale=False, tm=1024, tn=1024, name="norm_up_proj")
        xs = _residual_matmul(u, w_down[l].astype(BF16), xs, final_g.reshape(1, d),
                              final_norm=(l == depth - 1), tm=512, tk=D_MODEL, name="down_proj_residual")
    return xs.reshape(b, s, d)
```
